```python
import math
import jax, jax.numpy as jnp
from jax import lax
import numpy as np

D_MODEL = 2048
BATCH = 8
SEQ = 2048
DEPTH = 4
DEC_BATCH = 8
DEC_SEQ = 16
PAST_LEN = 4096

CHUNK = 64
N_MIXERS = 3
HEAD_DIM = 128
MIX_WIDTH = 3 * D_MODEL // 4
MIX_HEADS = MIX_WIDTH // HEAD_DIM
MEM_WIDTH = D_MODEL // 4
MEM_HEADS = MEM_WIDTH // HEAD_DIM
GATE_WIDTH = MIX_WIDTH + MEM_WIDTH
N_MEM = 256
Q_BLOCK = 128
A_KV_HEADS = 4
A_WINDOW = 128
A_PREV_CHUNKS = A_WINDOW // CHUNK
T5_BUCKETS = 32
T5_MAX_DIST = 128
B_Q_RANK = 512
B_KV_RANK = 256
B_NOPE = 128
B_ROPE = 64
B_V = 128
ROPE_THETA = 10000.0
C_PREV_CHUNKS = 8
C_REL_CLIP = 128
N_A = (DEPTH + 2) // 3
N_B = (DEPTH + 1) // 3
N_C = DEPTH // 3
EPS = 1e-6
NEG_INF = -1e30
A_SPLITS = (MIX_WIDTH, A_KV_HEADS * HEAD_DIM, A_KV_HEADS * HEAD_DIM, MEM_WIDTH, GATE_WIDTH)
B_SPLITS = (B_Q_RANK, B_KV_RANK, B_ROPE, MEM_WIDTH, GATE_WIDTH)
C_SPLITS = (MIX_WIDTH, MIX_WIDTH, MIX_WIDTH, MEM_WIDTH, GATE_WIDTH)

kernel_name = "hybrid_streaming_encoder_step"


def rmsnorm(x, g):
    xf = x.astype(jnp.float32)
    y = xf * lax.rsqrt(jnp.mean(xf * xf, axis=-1, keepdims=True) + EPS)
    return (y * g.astype(jnp.float32)).astype(x.dtype)


def split_cols(z, sizes):
    idx = np.cumsum(np.array(sizes))[:-1].tolist()
    return jnp.split(z, idx, axis=-1)


def attend(q, k, v, bias, valid, sink):
    b, nq, h, d = q.shape
    kvh = k.shape[2]
    g = h // kvh
    qg = q.reshape(b, nq, kvh, g, d)
    s = jnp.einsum("bqhgd,bkhd->bhgqk", qg, k, preferred_element_type=jnp.float32) * (d ** -0.5)
    if bias is not None:
        s = s + bias.astype(jnp.float32).reshape(kvh, g, nq, bias.shape[-1])
    if valid is not None:
        s = jnp.where(valid, s, NEG_INF)
    if sink is None:
        p = jax.nn.softmax(s, axis=-1)
    else:
        sk = sink.astype(jnp.float32).reshape(kvh, g, 1, 1)
        m = jnp.maximum(jnp.max(s, axis=-1, keepdims=True), sk)
        e = jnp.exp(s - m)
        p = e / (jnp.sum(e, axis=-1, keepdims=True) + jnp.exp(sk - m))
    o = jnp.einsum("bhgqk,bkhd->bqhgd", p.astype(v.dtype), v)
    return o.reshape(b, nq, h, v.shape[-1])


def t5_bucket(rel):
    half = T5_BUCKETS // 2
    max_exact = half // 2
    ret = jnp.where(rel < 0, half, 0)
    n = jnp.abs(rel)
    nf = jnp.maximum(n, 1).astype(jnp.float32)
    large = max_exact + (jnp.log(nf / max_exact) / math.log(T5_MAX_DIST / max_exact)
                         * (half - max_exact)).astype(jnp.int32)
    large = jnp.minimum(large, half - 1)
    return ret + jnp.where(n < max_exact, n, large)


def t5_rel_bias(table, rel):
    return jnp.moveaxis(jnp.take(table, t5_bucket(rel), axis=0), -1, 0)


def clipped_rel_bias(table, rel):
    return table[:, jnp.clip(rel, -C_REL_CLIP, C_REL_CLIP) + C_REL_CLIP]


def band_attention_prompt(q, k, v, n_prev, bias, sink):
    b, s, h, d = q.shape
    pad = n_prev * CHUNK
    band = pad + CHUNK
    n_chunks = s // CHUNK
    kp = jnp.pad(k, ((0, 0), (pad, 0), (0, 0), (0, 0)))
    vp = jnp.pad(v, ((0, 0), (pad, 0), (0, 0), (0, 0)))
    qc = jnp.moveaxis(q.reshape(b, n_chunks, CHUNK, h, d), 1, 0)
    k_off = jnp.arange(band) - pad

    def one(args):
        c, qb = args
        start = c * CHUNK
        kb = lax.dynamic_slice_in_dim(kp, start, band, axis=1)
        vb = lax.dynamic_slice_in_dim(vp, start, band, axis=1)
        valid = (start + k_off >= 0)[None, :]
        return attend(qb, kb, vb, bias, valid, sink)

    out = lax.map(one, (jnp.arange(n_chunks), qc))
    return jnp.moveaxis(out, 0, 1).reshape(b, s, h, v.shape[-1])


def band_mix(q, k, v, past_k, past_v, n_prev, bias_fn, sink):
    s = q.shape[1]
    if past_k is None:
        pad = n_prev * CHUNK
        rel = jnp.arange(CHUNK)[:, None] - (jnp.arange(pad + CHUNK) - pad)[None, :]
        out = band_attention_prompt(q, k, v, n_prev, bias_fn(rel), sink)
        keep = min(pad, s)
        return out, k[:, s - keep:], v[:, s - keep:]
    p = past_k.shape[1]
    rel = (p + jnp.arange(s))[:, None] - jnp.arange(p + s)[None, :]
    out = attend(q, jnp.concatenate([past_k, k], axis=1), jnp.concatenate([past_v, v], axis=1),
                 bias_fn(rel), None, sink)
    return out, k, v


def rope_cos_sin(pos):
    half = B_ROPE // 2
    inv = ROPE_THETA ** (-jnp.arange(half, dtype=jnp.float32) / half)
    ang = pos.astype(jnp.float32)[:, None] * inv[None, :]
    return jnp.cos(ang), jnp.sin(ang)


def apply_rope(x, cos, sin):
    xf = x.astype(jnp.float32)
    x1, x2 = jnp.split(xf, 2, axis=-1)
    return jnp.concatenate([x1 * cos - x2 * sin, x1 * sin + x2 * cos], axis=-1).astype(x.dtype)


def mla_keys(c_kv, k_rope, w_kv_b, g_k):
    b, n, _ = c_kv.shape
    kv = (c_kv @ w_kv_b).reshape(b, n, MIX_HEADS, B_NOPE + B_V)
    k_nope, v = kv[..., :B_NOPE], kv[..., B_NOPE:]
    k = jnp.concatenate([k_nope, jnp.broadcast_to(k_rope[:, :, None, :], (b, n, MIX_HEADS, B_ROPE))], axis=-1)
    return rmsnorm(k, g_k), v


def mla_prompt_attend(q, k, v):
    b, s, h, d = q.shape
    nb = s // Q_BLOCK
    qb = jnp.moveaxis(q.reshape(b, nb, Q_BLOCK, h, d), 1, 0)
    k_chunk = jnp.arange(s) // CHUNK

    def one(args):
        i, qblk = args
        q_chunk = (i * Q_BLOCK + jnp.arange(Q_BLOCK)) // CHUNK
        valid = k_chunk[None, :] <= q_chunk[:, None]
        return attend(qblk, k, v, None, valid, None)

    out = lax.map(one, (jnp.arange(nb), qb))
    return jnp.moveaxis(out, 0, 1).reshape(b, s, h, v.shape[-1])


def memory_kv(mem, g_mem, w_mem_kv, g_xk):
    b, n, _ = mem.shape
    kv = rmsnorm(mem, g_mem) @ w_mem_kv
    k = rmsnorm(kv[..., :MEM_WIDTH].reshape(b, n, MEM_HEADS, HEAD_DIM), g_xk)
    v = kv[..., MEM_WIDTH:].reshape(b, n, MEM_HEADS, HEAD_DIM)
    return k, v


def merge_and_project(x, mix_out, xq, gate, mem_k, mem_v, g_xq, w_out):
    b, s, _ = x.shape
    xq = rmsnorm(xq.reshape(b, s, MEM_HEADS, HEAD_DIM), g_xq)
    x_out = attend(xq, mem_k, mem_v, None, None, None)
    o = jnp.concatenate([mix_out.reshape(b, s, MIX_WIDTH), x_out.reshape(b, s, MEM_WIDTH)], axis=-1)
    return x + (o * jax.nn.silu(gate)) @ w_out


def layer_a(x, past_k, past_v, g_norm, w_in, g_q, g_k, sink, t5_table, mem_k, mem_v, g_xq, w_out):
    b, s, _ = x.shape
    q, k, v, xq, gate = split_cols(rmsnorm(x, g_norm) @ w_in, A_SPLITS)
    q = rmsnorm(q.reshape(b, s, MIX_HEADS, HEAD_DIM), g_q)
    k = rmsnorm(k.reshape(b, s, A_KV_HEADS, HEAD_DIM), g_k)
    v = v.reshape(b, s, A_KV_HEADS, HEAD_DIM)
    out, new_k, new_v = band_mix(q, k, v, past_k, past_v, A_PREV_CHUNKS,
                                 lambda rel: t5_rel_bias(t5_table, rel), sink)
    y = merge_and_project(x, out, xq, gate, mem_k, mem_v, g_xq, w_out)
    return y, new_k, new_v


def layer_b(x, pos, past_ckv, past_krope, g_norm, w_in, g_cq, w_q_b, g_ckv, w_kv_b, g_q, g_k,
            mem_k, mem_v, g_xq, w_out):
    b, s, _ = x.shape
    c_q, c_kv, k_rope, xq, gate = split_cols(rmsnorm(x, g_norm) @ w_in, B_SPLITS)
    cos, sin = rope_cos_sin(pos)
    q = (rmsnorm(c_q, g_cq) @ w_q_b).reshape(b, s, MIX_HEADS, B_NOPE + B_ROPE)
    q = jnp.concatenate([q[..., :B_NOPE], apply_rope(q[..., B_NOPE:], cos[:, None, :], sin[:, None, :])], axis=-1)
    q = rmsnorm(q, g_q)
    c_kv = rmsnorm(c_kv, g_ckv)
    k_rope = apply_rope(k_rope, cos, sin)
    if past_ckv is None:
        k, v = mla_keys(c_kv, k_rope, w_kv_b, g_k)
        out = mla_prompt_attend(q, k, v)
    else:
        k, v = mla_keys(jnp.concatenate([past_ckv, c_kv], axis=1),
                        jnp.concatenate([past_krope, k_rope], axis=1), w_kv_b, g_k)
        out = attend(q, k, v, None, None, None)
    y = merge_and_project(x, out, xq, gate, mem_k, mem_v, g_xq, w_out)
    return y, c_kv, k_rope


def layer_c(x, past_k, past_v, g_norm, w_in, g_q, g_k, rel_table, mem_k, mem_v, g_xq, w_out):
    b, s, _ = x.shape
    q, k, v, xq, gate = split_cols(rmsnorm(x, g_norm) @ w_in, C_SPLITS)
    q = rmsnorm(q.reshape(b, s, MIX_HEADS, HEAD_DIM), g_q)
    k = rmsnorm(k.reshape(b, s, MIX_HEADS, HEAD_DIM), g_k)
    v = v.reshape(b, s, MIX_HEADS, HEAD_DIM)
    out, new_k, new_v = band_mix(q, k, v, past_k, past_v, C_PREV_CHUNKS,
                                 lambda rel: clipped_rel_bias(rel_table, rel), None)
    y = merge_and_project(x, out, xq, gate, mem_k, mem_v, g_xq, w_out)
    return y, new_k, new_v


def setup_inputs(seed: int = 0) -> dict:
    key = jax.random.key(seed)
    keys = iter(jax.random.split(key, 40))

    def nrm(shape, scale):
        return jax.random.normal(next(keys), shape, jnp.float32) * scale

    def gain(shape):
        return 1.0 + nrm(shape, 0.02)

    a_rows = min(A_WINDOW, PAST_LEN)
    c_rows = min(C_PREV_CHUNKS * CHUNK, PAST_LEN)
    d_in = D_MODEL ** -0.5
    return {
        "x_prompt": nrm((BATCH, SEQ, D_MODEL), 1.0),
        "x_sample": nrm((DEC_BATCH, DEC_SEQ, D_MODEL), 1.0),
        "mem_prompt": nrm((BATCH, N_MEM, D_MODEL), 1.0),
        "cache_a_k": nrm((N_A, DEC_BATCH, a_rows, A_KV_HEADS, HEAD_DIM), 1.0),
        "cache_a_v": nrm((N_A, DEC_BATCH, a_rows, A_KV_HEADS, HEAD_DIM), 1.0),
        "cache_b_ckv": nrm((N_B, DEC_BATCH, PAST_LEN, B_KV_RANK), 1.0),
        "cache_b_krope": nrm((N_B, DEC_BATCH, PAST_LEN, B_ROPE), 1.0),
        "cache_c_k": nrm((N_C, DEC_BATCH, c_rows, MIX_HEADS, HEAD_DIM), 1.0),
        "cache_c_v": nrm((N_C, DEC_BATCH, c_rows, MIX_HEADS, HEAD_DIM), 1.0),
        "cache_mem_k": nrm((DEPTH, DEC_BATCH, N_MEM, MEM_HEADS, HEAD_DIM), 1.0),
        "cache_mem_v": nrm((DEPTH, DEC_BATCH, N_MEM, MEM_HEADS, HEAD_DIM), 1.0),
        "t5_bias": nrm((T5_BUCKETS, MIX_HEADS), 0.5),
        "norm_g": gain((DEPTH, D_MODEL)),
        "w_out": nrm((DEPTH, GATE_WIDTH, D_MODEL), GATE_WIDTH ** -0.5),
        "mem_norm_g": gain((DEPTH, D_MODEL)),
        "w_mem_kv": nrm((DEPTH, D_MODEL, 2 * MEM_WIDTH), d_in),
        "xq_norm_g": gain((DEPTH, HEAD_DIM)),
        "xk_norm_g": gain((DEPTH, HEAD_DIM)),
        "a_w_in": nrm((N_A, D_MODEL, sum(A_SPLITS)), d_in),
        "a_q_norm_g": gain((N_A, HEAD_DIM)),
        "a_k_norm_g": gain((N_A, HEAD_DIM)),
        "a_sink": nrm((N_A, MIX_HEADS), 0.5),
        "b_w_in": nrm((N_B, D_MODEL, sum(B_SPLITS)), d_in),
        "b_cq_norm_g": gain((N_B, B_Q_RANK)),
        "b_w_q_b": nrm((N_B, B_Q_RANK, MIX_HEADS * (B_NOPE + B_ROPE)), B_Q_RANK ** -0.5),
        "b_ckv_norm_g": gain((N_B, B_KV_RANK)),
        "b_w_kv_b": nrm((N_B, B_KV_RANK, MIX_HEADS * (B_NOPE + B_V)), B_KV_RANK ** -0.5),
        "b_q_norm_g": gain((N_B, B_NOPE + B_ROPE)),
        "b_k_norm_g": gain((N_B, B_NOPE + B_ROPE)),
        "c_w_in": nrm((N_C, D_MODEL, sum(C_SPLITS)), d_in),
        "c_q_norm_g": gain((N_C, HEAD_DIM)),
        "c_k_norm_g": gain((N_C, HEAD_DIM)),
        "c_rel_bias": nrm((N_C, MIX_HEADS, 2 * C_REL_CLIP + 1), 0.5),
    }


def reference(x_prompt, x_sample, mem_prompt, cache_a_k, cache_a_v, cache_b_ckv, cache_b_krope,
              cache_c_k, cache_c_v, cache_mem_k, cache_mem_v, t5_bias, norm_g, w_out, mem_norm_g,
              w_mem_kv, xq_norm_g, xk_norm_g, a_w_in, a_q_norm_g, a_k_norm_g, a_sink, b_w_in,
              b_cq_norm_g, b_w_q_b, b_ckv_norm_g, b_w_kv_b, b_q_norm_g, b_k_norm_g, c_w_in,
              c_q_norm_g, c_k_norm_g, c_rel_bias):
    past = cache_b_ckv.shape[2]
    pos_p = jnp.arange(x_prompt.shape[1])
    pos_s = past + jnp.arange(x_sample.shape[1])
    yp, ys = x_prompt, x_sample
    a_kp, a_vp, a_ks, a_vs = [], [], [], []
    b_cp, b_rp, b_cs, b_rs = [], [], [], []
    c_kp, c_vp, c_ks, c_vs = [], [], [], []
    m_k, m_v = [], []
    for i in range(DEPTH):
        kind, j = i % N_MIXERS, i // N_MIXERS
        mk, mv = memory_kv(mem_prompt, mem_norm_g[i], w_mem_kv[i], xk_norm_g[i])
        m_k.append(mk)
        m_v.append(mv)
        if kind == 0:
            yp, kn, vn = layer_a(yp, None, None, norm_g[i], a_w_in[j], a_q_norm_g[j], a_k_norm_g[j],
                                 a_sink[j], t5_bias, mk, mv, xq_norm_g[i], w_out[i])
            a_kp.append(kn)
            a_vp.append(vn)
            ys, kn, vn = layer_a(ys, cache_a_k[j], cache_a_v[j], norm_g[i], a_w_in[j], a_q_norm_g[j],
                                 a_k_norm_g[j], a_sink[j], t5_bias, cache_mem_k[i], cache_mem_v[i],
                                 xq_norm_g[i], w_out[i])
            a_ks.append(kn)
            a_vs.append(vn)
        elif kind == 1:
            yp, cn, rn = layer_b(yp, pos_p, None, None, norm_g[i], b_w_in[j], b_cq_norm_g[j], b_w_q_b[j],
                                 b_ckv_norm_g[j], b_w_kv_b[j], b_q_norm_g[j], b_k_norm_g[j],
                                 mk, mv, xq_norm_g[i], w_out[i])
            b_cp.append(cn)
            b_rp.append(rn)
            ys, cn, rn = layer_b(ys, pos_s, cache_b_ckv[j], cache_b_krope[j], norm_g[i], b_w_in[j],
                                 b_cq_norm_g[j], b_w_q_b[j], b_ckv_norm_g[j], b_w_kv_b[j],
                                 b_q_norm_g[j], b_k_norm_g[j], cache_mem_k[i], cache_mem_v[i],
                                 xq_norm_g[i], w_out[i])
            b_cs.append(cn)
            b_rs.append(rn)
        else:
            yp, kn, vn = layer_c(yp, None, None, norm_g[i], c_w_in[j], c_q_norm_g[j], c_k_norm_g[j],
                                 c_rel_bias[j], mk, mv, xq_norm_g[i], w_out[i])
            c_kp.append(kn)
            c_vp.append(vn)
            ys, kn, vn = layer_c(ys, cache_c_k[j], cache_c_v[j], norm_g[i], c_w_in[j], c_q_norm_g[j],
                                 c_k_norm_g[j], c_rel_bias[j], cache_mem_k[i], cache_mem_v[i],
                                 xq_norm_g[i], w_out[i])
            c_ks.append(kn)
            c_vs.append(vn)
    return (yp, ys,
            jnp.stack(a_kp), jnp.stack(a_vp), jnp.stack(a_ks), jnp.stack(a_vs),
            jnp.stack(b_cp), jnp.stack(b_rp), jnp.stack(b_cs), jnp.stack(b_rs),
            jnp.stack(c_kp), jnp.stack(c_vp), jnp.stack(c_ks), jnp.stack(c_vs),
            jnp.stack(m_k), jnp.stack(m_v))
```

```python
import functools
import math
from typing import NamedTuple

import numpy as np
import jax
import jax.numpy as jnp
from jax import lax
from jax.experimental import pallas as pl
from jax.experimental.pallas import tpu as pltpu

BF = jnp.bfloat16
F32 = jnp.float32

HEAD_DIM = 128
CHUNK = 64
N_MIXERS = 3
MIX_HEADS = 12
MEM_HEADS = 4
A_KV_HEADS = 4
A_PREV_CHUNKS = 2
C_PREV_CHUNKS = 8
C_REL_CLIP = 128
T5_BUCKETS = 32
T5_MAX_DIST = 128
B_Q_RANK = 512
B_KV_RANK = 256
B_NOPE = 128
B_ROPE = 64
B_V = 128
ROPE_THETA = 10000.0
EPS = 1e-6
NEG_INF = -1e30

V7X_VMEM_BYTES = 64 * 1024 * 1024
VMEM_LIMIT_BYTES = V7X_VMEM_BYTES - 8 * 1024 * 1024
LANES = 128
PROJ_TILE_ROWS = 256
MLA_Q_BLOCK = 256
MLA_KEY_CHUNK = 1024


def _params():
    return pltpu.CompilerParams(dimension_semantics=("arbitrary",),
                                vmem_limit_bytes=VMEM_LIMIT_BYTES)


def _params2():
    return pltpu.CompilerParams(dimension_semantics=("arbitrary", "arbitrary"),
                                vmem_limit_bytes=VMEM_LIMIT_BYTES)


def _resident(shape):
    nd = len(shape)
    return pl.BlockSpec(shape, lambda *_: (0,) * nd, pipeline_mode=pl.Buffered(1))


def _dot(a, b):
    return jnp.dot(a, b, preferred_element_type=F32)


def _dot_t(a, b):
    return lax.dot_general(a, b, (((1,), (1,)), ((), ())), preferred_element_type=F32)


class Group(NamedTuple):
    width: int
    epi: str
    bf16_out: bool
    f32_keep: int


def _proj_body(*refs, groups, tm, tiles_per_batch, has_rope):
    it = iter(refs)
    x_ref = next(it)
    g_ref = next(it)
    cos_ref = sin_ref = None
    if has_rope:
        cos_ref = next(it)
        sin_ref = next(it)
    gin = []
    for gr in groups:
        w = next(it)
        wrot = next(it) if gr.epi == "rope" else None
        gain = next(it) if gr.epi in ("headnorm", "fullnorm") else None
        gin.append((w, wrot, gain))
    gout = []
    for gr in groups:
        ob = next(it) if gr.bf16_out else None
        of = next(it) if gr.f32_keep else None
        gout.append((ob, of))

    x = x_ref[...]
    ms = jnp.mean(x * x, axis=-1, keepdims=True)
    xn = (x * lax.rsqrt(ms + EPS) * g_ref[...]).astype(BF)
    t = pl.program_id(0) % tiles_per_batch

    for gr, (w, wrot, gain), (ob, of) in zip(groups, gin, gout):
        if gr.epi in ("fullnorm", "rope"):
            chunk = gr.width
        else:
            chunk = min(gr.width, 4 * HEAD_DIM)
        for c0 in range(0, gr.width, chunk):
            z = _dot(xn, w[:, c0:c0 + chunk])
            if gr.epi == "headnorm":
                parts = []
                for h in range(chunk // HEAD_DIM):
                    zh = z[:, h * HEAD_DIM:(h + 1) * HEAD_DIM]
                    r = lax.rsqrt(jnp.mean(zh * zh, axis=-1, keepdims=True) + EPS)
                    parts.append(zh * r)
                z = jnp.concatenate(parts, axis=-1) * gain[:, c0:c0 + chunk]
            elif gr.epi == "fullnorm":
                r = lax.rsqrt(jnp.mean(z * z, axis=-1, keepdims=True) + EPS)
                z = z * r * gain[...]
            elif gr.epi == "rope":
                z = z * cos_ref[...] + _dot(xn, wrot[...]) * sin_ref[...]
            elif gr.epi == "silu":
                z = z * jax.nn.sigmoid(z)
            if ob is not None:
                ob[:, c0:c0 + chunk] = z.astype(BF)
            if of is not None:
                kb = min(gr.f32_keep, tm)
                nkb = gr.f32_keep // kb
                if kb == tm and nkb == tiles_per_batch:
                    of[:, c0:c0 + chunk] = z
                elif kb == tm:
                    @pl.when(t >= tiles_per_batch - nkb)
                    def _(of=of, z=z, c0=c0, chunk=chunk):
                        of[:, c0:c0 + chunk] = z
                else:
                    @pl.when(t == tiles_per_batch - 1)
                    def _(of=of, z=z, c0=c0, chunk=chunk, kb=kb):
                        of[:, c0:c0 + chunk] = z[tm - kb:, :]


def _project(x2, g, groups, weights, *, rows_per_batch, tm, cos=None, sin=None):
    m, d = x2.shape
    assert m % tm == 0 and rows_per_batch % tm == 0
    tiles_per_batch = rows_per_batch // tm
    n_batches = m // rows_per_batch
    has_rope = cos is not None

    inputs = [x2, g.reshape(1, d).astype(F32)]
    in_specs = [pl.BlockSpec((tm, d), lambda i: (i, 0)), _resident((1, d))]
    if has_rope:
        assert cos.shape[0] == rows_per_batch
        inputs += [cos, sin]
        in_specs += [pl.BlockSpec((tm, cos.shape[1]), lambda i: (i % tiles_per_batch, 0))] * 2
    for gr, ws in zip(groups, weights):
        for a in ws:
            inputs.append(a)
            in_specs.append(_resident(a.shape))

    out_shapes, out_specs = [], []
    for gr in groups:
        if gr.bf16_out:
            out_shapes.append(jax.ShapeDtypeStruct((m, gr.width), BF))
            out_specs.append(pl.BlockSpec((tm, gr.width), lambda i: (i, 0)))
        if gr.f32_keep:
            keep = gr.f32_keep
            assert keep <= rows_per_batch
            kb = min(keep, tm)
            assert keep % kb == 0 and (kb == tm or keep == kb)
            nkb = keep // kb

            def f32_map(i, nkb=nkb):
                b = i // tiles_per_batch
                t = i % tiles_per_batch
                return (b * nkb + jnp.maximum(t - (tiles_per_batch - nkb), 0), 0)

            out_shapes.append(jax.ShapeDtypeStruct((n_batches * keep, gr.width), F32))
            out_specs.append(pl.BlockSpec((kb, gr.width), f32_map))

    body = functools.partial(_proj_body, groups=tuple(groups), tm=tm,
                             tiles_per_batch=tiles_per_batch, has_rope=has_rope)
    return pl.pallas_call(
        body, grid=(m // tm,), in_specs=in_specs, out_specs=out_specs, out_shape=out_shapes,
        compiler_params=_params(), name="proj")(*inputs)


def _band_body(*refs, nb, heads, kv_heads, has_sink):
    if has_sink:
        q_ref, k_ref, v_ref, bias_ref, sink_ref, o_ref = refs
    else:
        q_ref, k_ref, v_ref, bias_ref, o_ref = refs
        sink_ref = None
    i = pl.program_id(1)
    start_blk = jnp.maximum(i - (nb - 1), 0)
    shift = (nb - 1) - (i - start_blk)
    row0 = pl.multiple_of(start_blk * LANES, LANES)
    group = heads // kv_heads
    for g in range(kv_heads):
        kg = k_ref[0, pl.ds(row0, nb * LANES), g * HEAD_DIM:(g + 1) * HEAD_DIM]
        vg = v_ref[0, pl.ds(row0, nb * LANES), g * HEAD_DIM:(g + 1) * HEAD_DIM]
        for hh in range(group):
            h = g * group + hh
            qh = q_ref[0, :, h * HEAD_DIM:(h + 1) * HEAD_DIM]
            s = _dot_t(qh, kg)
            bias = jnp.concatenate(
                [bias_ref[jnp.minimum(shift + kb, nb), h] for kb in range(nb)], axis=1)
            s = s + bias
            m = jnp.max(s, axis=-1, keepdims=True)
            if has_sink:
                sk = sink_ref[h]
                m = jnp.maximum(m, sk)
            e = jnp.exp(s - m)
            den = jnp.sum(e, axis=-1, keepdims=True)
            if has_sink:
                den = den + jnp.exp(sk - m)
            o = _dot(e.astype(BF), vg) * (1.0 / den)
            o_ref[0, :, h * HEAD_DIM:(h + 1) * HEAD_DIM] = o.astype(BF)


def _band_bias_blocks(bias_of_rel, n_prev, nb):
    r = np.arange(LANES)[:, None]
    j = np.arange(nb * LANES)[None, :]
    k_off = j - (nb - 1) * LANES
    rel = r - k_off
    qc = r // CHUNK
    kc = np.floor_divide(k_off, CHUNK)
    valid = (qc - kc >= 0) & (qc - kc <= n_prev)
    b = bias_of_rel(rel)
    b = jnp.where(jnp.asarray(valid)[None], b.astype(F32), NEG_INF)
    h = b.shape[0]
    b = b.reshape(h, LANES, nb, LANES).transpose(2, 0, 1, 3)
    return jnp.concatenate([b, jnp.full((1, h, LANES, LANES), NEG_INF, F32)], axis=0)


def _band_attention(q, k, v, bias_blocks, sink, *, nb, kv_heads):
    b, s, qw = q.shape
    kvw = k.shape[2]
    heads = qw // HEAD_DIM
    assert s % LANES == 0 and s >= nb * LANES
    inputs = [q, k, v, bias_blocks]
    in_specs = [
        pl.BlockSpec((1, LANES, qw), lambda bi, i: (bi, i, 0)),
        pl.BlockSpec((1, s, kvw), lambda bi, i: (bi, 0, 0)),
        pl.BlockSpec((1, s, kvw), lambda bi, i: (bi, 0, 0)),
        _resident(bias_blocks.shape),
    ]
    if sink is not None:
        inputs.append(sink.reshape(heads, 1, 1).astype(F32))
        in_specs.append(_resident((heads, 1, 1)))
    body = functools.partial(_band_body, nb=nb, heads=heads, kv_heads=kv_heads,
                             has_sink=sink is not None)
    return pl.pallas_call(
        body, grid=(b, s // LANES), in_specs=in_specs,
        out_specs=pl.BlockSpec((1, LANES, qw), lambda bi, i: (bi, i, 0)),
        out_shape=jax.ShapeDtypeStruct((b, s, qw), BF),
        compiler_params=_params2(), name="band_attn")(*inputs)


def _dense_body(*refs, heads, kv_heads, has_sink):
    if has_sink:
        q_ref, k_ref, v_ref, bias_ref, sink_ref, o_ref = refs
    else:
        q_ref, k_ref, v_ref, bias_ref, o_ref = refs
        sink_ref = None
    group = heads // kv_heads
    for g in range(kv_heads):
        kg = k_ref[0, :, g * HEAD_DIM:(g + 1) * HEAD_DIM]
        vg = v_ref[0, :, g * HEAD_DIM:(g + 1) * HEAD_DIM]
        for hh in range(group):
            h = g * group + hh
            qh = q_ref[0, :, h * HEAD_DIM:(h + 1) * HEAD_DIM]
            s = _dot_t(qh, kg) + bias_ref[h]
            m = jnp.max(s, axis=-1, keepdims=True)
            if has_sink:
                sk = sink_ref[h]
                m = jnp.maximum(m, sk)
            e = jnp.exp(s - m)
            den = jnp.sum(e, axis=-1, keepdims=True)
            if has_sink:
                den = den + jnp.exp(sk - m)
            o = _dot(e.astype(BF), vg) * (1.0 / den)
            o_ref[0, :, h * HEAD_DIM:(h + 1) * HEAD_DIM] = o.astype(BF)


def _dense_attention(q, k, v, bias, sink, *, kv_heads):
    b, sq, qw = q.shape
    sk, kvw = k.shape[1], k.shape[2]
    heads = qw // HEAD_DIM
    inputs = [q, k, v, bias]
    in_specs = [
        pl.BlockSpec((1, sq, qw), lambda bi: (bi, 0, 0)),
        pl.BlockSpec((1, sk, kvw), lambda bi: (bi, 0, 0)),
        pl.BlockSpec((1, sk, kvw), lambda bi: (bi, 0, 0)),
        _resident(bias.shape),
    ]
    if sink is not None:
        inputs.append(sink.reshape(heads, 1, 1).astype(F32))
        in_specs.append(_resident((heads, 1, 1)))
    body = functools.partial(_dense_body, heads=heads, kv_heads=kv_heads,
                             has_sink=sink is not None)
    return pl.pallas_call(
        body, grid=(b,), in_specs=in_specs,
        out_specs=pl.BlockSpec((1, sq, qw), lambda bi: (bi, 0, 0)),
        out_shape=jax.ShapeDtypeStruct((b, sq, qw), BF),
        compiler_params=_params(), name="dense_attn")(*inputs)


def _mla_queries(cq, wqn, wqr, wqrr, cos, sin, gq):
    qn = _dot(cq, wqn)
    qr = _dot(cq, wqr) * cos + _dot(cq, wqrr) * sin
    ms = (jnp.sum(qn * qn, axis=-1, keepdims=True)
          + jnp.sum(qr * qr, axis=-1, keepdims=True)) * (1.0 / (B_NOPE + B_ROPE))
    r = lax.rsqrt(ms + EPS)
    return jnp.concatenate([qn * r * gq[:, :B_NOPE], qr * r * gq[:, B_NOPE:]], axis=-1).astype(BF)


def _mla_keys_values(ckv, kr, wk, wv, gk):
    kn = _dot(ckv, wk)
    ms = (jnp.sum(kn * kn, axis=-1, keepdims=True)
          + jnp.sum(kr * kr, axis=-1, keepdims=True)) * (1.0 / (B_NOPE + B_ROPE))
    r = lax.rsqrt(ms + EPS)
    k = jnp.concatenate([kn * r * gk[:, :B_NOPE], kr * r * gk[:, B_NOPE:]], axis=-1).astype(BF)
    return k, _dot(ckv, wv).astype(BF)


def _mla_prompt_body(cq_ref, ckv_ref, kr_ref, cos_ref, sin_ref, wqn_ref, wqr_ref, wqrr_ref,
                     wk_ref, wv_ref, gq_ref, gk_ref, o_ref, q_s, k_s, v_s, *, seq):
    qb = MLA_Q_BLOCK
    for r0 in range(0, seq, qb):
        rows = slice(r0, r0 + qb)
        q_s[rows, :] = _mla_queries(cq_ref[0, rows, :], wqn_ref[0], wqr_ref[0], wqrr_ref[0],
                                    cos_ref[rows, :], sin_ref[rows, :], gq_ref[...])
        k, v = _mla_keys_values(ckv_ref[0, rows, :], kr_ref[0, rows, :].astype(F32),
                                wk_ref[0], wv_ref[0], gk_ref[...])
        k_s[rows, :] = k
        v_s[rows, :] = v
    chunk_shift = CHUNK.bit_length() - 1
    ri = lax.shift_right_logical(lax.broadcasted_iota(jnp.int32, (qb, qb), 0), chunk_shift)
    ci = lax.shift_right_logical(lax.broadcasted_iota(jnp.int32, (qb, qb), 1), chunk_shift)
    diag_bias = jnp.where(ci <= ri, 0.0, NEG_INF).astype(F32)
    for r0 in range(0, seq, qb):
        rows = slice(r0, r0 + qb)
        q = q_s[rows, :]
        s_d = _dot_t(q, k_s[rows, :]) + diag_bias
        m = jnp.max(s_d, axis=-1, keepdims=True)
        if r0 > 0:
            s_o = _dot_t(q, k_s[0:r0, :])
            m = jnp.maximum(m, jnp.max(s_o, axis=-1, keepdims=True))
        e_d = jnp.exp(s_d - m)
        den = jnp.sum(e_d, axis=-1, keepdims=True)
        o = _dot(e_d.astype(BF), v_s[rows, :])
        if r0 > 0:
            e_o = jnp.exp(s_o - m)
            den = den + jnp.sum(e_o, axis=-1, keepdims=True)
            o = o + _dot(e_o.astype(BF), v_s[0:r0, :])
        o_ref[0, rows, :] = (o * (1.0 / den)).astype(BF)


def _mla_prompt(cq, ckv, kr, cos, sin, w, gq, gk):
    b, s, _ = cq.shape
    assert s % MLA_Q_BLOCK == 0
    wqn, wqr, wqrr, wk, wv = w
    per_head = lambda a: pl.BlockSpec((1,) + a.shape[1:], lambda bi, h: (h, 0, 0))
    per_batch = lambda a: pl.BlockSpec((1,) + a.shape[1:], lambda bi, h: (bi, 0, 0))
    in_specs = [per_batch(cq), per_batch(ckv), per_batch(kr), _resident(cos.shape),
                _resident(sin.shape), per_head(wqn), per_head(wqr), per_head(wqrr), per_head(wk),
                per_head(wv), _resident(gq.shape), _resident(gk.shape)]
    return pl.pallas_call(
        functools.partial(_mla_prompt_body, seq=s),
        grid=(b, MIX_HEADS), in_specs=in_specs,
        out_specs=pl.BlockSpec((1, s, B_V), lambda bi, h: (bi, 0, h)),
        out_shape=jax.ShapeDtypeStruct((b, s, MIX_HEADS * B_V), BF),
        scratch_shapes=[pltpu.VMEM((s, B_NOPE + B_ROPE), BF), pltpu.VMEM((s, B_NOPE + B_ROPE), BF),
                        pltpu.VMEM((s, B_V), BF)],
        compiler_params=_params2(), name="mla_prompt")(cq, ckv, kr, cos, sin, wqn, wqr, wqrr, wk,
                                                        wv, gq, gk)


def _mla_sample_body(cq_ref, pckv_ref, pkr_ref, nckv_ref, nkr_ref, cos_ref, sin_ref, wqn_ref,
                     wqr_ref, wqrr_ref, wk_ref, wv_ref, gq_ref, gk_ref, o_ref, *, past):
    q = _mla_queries(cq_ref[0], wqn_ref[0], wqr_ref[0], wqrr_ref[0], cos_ref[...], sin_ref[...],
                     gq_ref[...])
    pieces = []
    for r0 in range(0, past, MLA_KEY_CHUNK):
        rows = slice(r0, min(r0 + MLA_KEY_CHUNK, past))
        k, v = _mla_keys_values(pckv_ref[0, rows, :].astype(BF), pkr_ref[0, rows, :],
                                wk_ref[0], wv_ref[0], gk_ref[...])
        pieces.append((_dot_t(q, k), v))
    k, v = _mla_keys_values(nckv_ref[0].astype(BF), nkr_ref[0], wk_ref[0], wv_ref[0], gk_ref[...])
    pieces.append((_dot_t(q, k), v))
    m = functools.reduce(jnp.maximum, [jnp.max(s, axis=-1, keepdims=True) for s, _ in pieces])
    den = 0.0
    o = 0.0
    for s, v in pieces:
        e = jnp.exp(s - m)
        den = den + jnp.sum(e, axis=-1, keepdims=True)
        o = o + _dot(e.astype(BF), v)
    o_ref[0] = (o * (1.0 / den)).astype(BF)


def _mla_sample(cq, past_ckv, past_kr, new_ckv, new_kr, cos, sin, w, gq, gk):
    b, n, _ = cq.shape
    past = past_ckv.shape[1]
    wqn, wqr, wqrr, wk, wv = w
    per_head = lambda a: pl.BlockSpec((1,) + a.shape[1:], lambda bi, h: (h, 0, 0))
    per_batch = lambda a: pl.BlockSpec((1,) + a.shape[1:], lambda bi, h: (bi, 0, 0))
    in_specs = [per_batch(cq), per_batch(past_ckv), per_batch(past_kr), per_batch(new_ckv),
                per_batch(new_kr), _resident(cos.shape), _resident(sin.shape), per_head(wqn),
                per_head(wqr), per_head(wqrr), per_head(wk), per_head(wv), _resident(gq.shape),
                _resident(gk.shape)]
    return pl.pallas_call(
        functools.partial(_mla_sample_body, past=past),
        grid=(b, MIX_HEADS), in_specs=in_specs,
        out_specs=pl.BlockSpec((1, n, B_V), lambda bi, h: (bi, 0, h)),
        out_shape=jax.ShapeDtypeStruct((b, n, MIX_HEADS * B_V), BF),
        compiler_params=_params2(), name="mla_sample")(cq, past_ckv, past_kr, new_ckv, new_kr, cos,
                                                        sin, wqn, wqr, wqrr, wk, wv, gq, gk)


def _merge_body(x_ref, mix_ref, xq_ref, sg_ref, mk_ref, mv_ref, w_ref, y_ref, *, nbat, rows):
    mix_w = mix_ref.shape[1]
    outs = []
    for bi in range(nbat):
        rs = slice(bi * rows, (bi + 1) * rows)
        heads = []
        for h in range(MEM_HEADS):
            cs = slice(h * HEAD_DIM, (h + 1) * HEAD_DIM)
            s = _dot_t(xq_ref[rs, cs], mk_ref[bi, :, cs])
            m = jnp.max(s, axis=-1, keepdims=True)
            e = jnp.exp(s - m)
            den = jnp.sum(e, axis=-1, keepdims=True)
            heads.append(_dot(e.astype(BF), mv_ref[bi, :, cs]) * (1.0 / den))
        outs.append(jnp.concatenate(heads, axis=-1))
    xo = outs[0] if nbat == 1 else jnp.concatenate(outs, axis=0)
    o1 = mix_ref[...] * sg_ref[:, :mix_w]
    o2 = xo.astype(BF) * sg_ref[:, mix_w:]
    y_ref[...] = x_ref[...] + _dot(o1, w_ref[:mix_w, :]) + _dot(o2, w_ref[mix_w:, :])


def _merge(x2, mix, xq, sg, mk, mv, w_out, *, rows_per_batch, tm):
    m, d = x2.shape
    if tm <= rows_per_batch:
        assert rows_per_batch % tm == 0
        tiles_per_batch = rows_per_batch // tm
        nbat, rows = 1, tm
        mem_map = lambda i: (i // tiles_per_batch, 0, 0)
    else:
        assert tm % rows_per_batch == 0
        nbat, rows = tm // rows_per_batch, rows_per_batch
        mem_map = lambda i: (i, 0, 0)
    row_spec = lambda a: pl.BlockSpec((tm, a.shape[1]), lambda i: (i, 0))
    mem_spec = pl.BlockSpec((nbat,) + mk.shape[1:], mem_map)
    return pl.pallas_call(
        functools.partial(_merge_body, nbat=nbat, rows=rows),
        grid=(m // tm,),
        in_specs=[row_spec(x2), row_spec(mix), row_spec(xq), row_spec(sg), mem_spec, mem_spec,
                  _resident(w_out.shape)],
        out_specs=pl.BlockSpec((tm, d), lambda i: (i, 0)),
        out_shape=jax.ShapeDtypeStruct((m, d), F32),
        compiler_params=_params(), name="merge")(x2, mix, xq, sg, mk, mv, w_out)


def _split_cols(w, sizes):
    offs = np.concatenate([[0], np.cumsum(sizes)])
    return [w[:, int(offs[i]):int(offs[i + 1])].astype(BF) for i in range(len(sizes))]


def _tiled_gain(g, heads, scale=1.0):
    return (jnp.tile(g.astype(F32), heads) * scale).reshape(1, -1)


def _t5_bucket_static(rel):
    half = T5_BUCKETS // 2
    max_exact = half // 2
    ret = np.where(rel < 0, half, 0)
    n = np.abs(rel)
    nf = np.maximum(n, 1).astype(np.float64)
    large = max_exact + (np.log(nf / max_exact) / math.log(T5_MAX_DIST / max_exact)
                         * (half - max_exact)).astype(np.int32)
    large = np.minimum(large, half - 1)
    return ret + np.where(n < max_exact, n, large)


def _t5_bias_of_rel(table):
    return lambda rel: jnp.moveaxis(jnp.take(table, jnp.asarray(_t5_bucket_static(rel)), axis=0),
                                    -1, 0)


def _clipped_bias_of_rel(table):
    return lambda rel: table[:, jnp.asarray(np.clip(rel, -C_REL_CLIP, C_REL_CLIP) + C_REL_CLIP)]


def _rope_tables(pos):
    half = B_ROPE // 2
    inv = ROPE_THETA ** (-jnp.arange(half, dtype=F32) / half)
    ang = pos.astype(F32)[:, None] * inv[None, :]
    cos, sin = jnp.cos(ang), jnp.sin(ang)
    return jnp.concatenate([cos, cos], axis=-1), jnp.concatenate([sin, sin], axis=-1)


def _rot_cols(w):
    shape = w.shape
    w = w.reshape(shape[:-1] + (-1, 2, B_ROPE // 2))
    w = jnp.stack([-w[..., 1, :], w[..., 0, :]], axis=-2)
    return w.reshape(shape)


def _sample_bias(bias_of_rel, past, n, pad_to):
    rel = (past + np.arange(n))[:, None] - np.arange(past + n)[None, :]
    b = bias_of_rel(rel).astype(F32)
    return jnp.pad(b, ((0, 0), (0, 0), (0, pad_to - (past + n))), constant_values=NEG_INF)


def _pad_rows(a, rows):
    return jnp.pad(a, ((0, 0), (0, rows - a.shape[1]), (0, 0)))


def _round_up(n, m):
    return (n + m - 1) // m * m


def kernel(x_prompt, x_sample, mem_prompt, cache_a_k, cache_a_v, cache_b_ckv, cache_b_krope, cache_c_k, cache_c_v, cache_mem_k, cache_mem_v, t5_bias, norm_g, w_out, mem_norm_g, w_mem_kv, xq_norm_g, xk_norm_g, a_w_in, a_q_norm_g, a_k_norm_g, a_sink, b_w_in, b_cq_norm_g, b_w_q_b, b_ckv_norm_g, b_w_kv_b, b_q_norm_g, b_k_norm_g, c_w_in, c_q_norm_g, c_k_norm_g, c_rel_bias):
    bp, sp, d = x_prompt.shape
    bs, ss, _ = x_sample.shape
    depth = norm_g.shape[0]
    n_mem = mem_prompt.shape[1]
    past = cache_b_ckv.shape[2]
    mix_w = MIX_HEADS * HEAD_DIM
    mem_w = MEM_HEADS * HEAD_DIM
    gate_w = mix_w + mem_w
    attn_scale = HEAD_DIM ** -0.5
    mla_scale = (B_NOPE + B_ROPE) ** -0.5
    tm_p = min(PROJ_TILE_ROWS, sp)
    tm_m = min(PROJ_TILE_ROWS, n_mem)
    ms = bs * ss

    yp = x_prompt.reshape(bp * sp, d)
    ys = x_sample.reshape(ms, d)
    mem2 = mem_prompt.reshape(bp * n_mem, d)

    cos_p, sin_p = _rope_tables(jnp.arange(sp))
    cos_s, sin_s = _rope_tables(past + jnp.arange(ss))
    cos_s_all, sin_s_all = jnp.tile(cos_s, (bs, 1)), jnp.tile(sin_s, (bs, 1))

    outs = {k: [] for k in ("a_kp", "a_vp", "a_ks", "a_vs", "b_cp", "b_rp", "b_cs", "b_rs",
                            "c_kp", "c_vp", "c_ks", "c_vs", "m_k", "m_v")}

    for i in range(depth):
        kind, j = i % N_MIXERS, i // N_MIXERS
        w_o = w_out[i].astype(BF)
        g_xq = _tiled_gain(xq_norm_g[i], MEM_HEADS, attn_scale)

        wmk, wmv = _split_cols(w_mem_kv[i], (mem_w, mem_w))
        mem_groups = [Group(mem_w, "headnorm", True, n_mem), Group(mem_w, "none", True, n_mem)]
        mkb, mkf, mvb, mvf = _project(
            mem2, mem_norm_g[i], mem_groups, [(wmk, _tiled_gain(xk_norm_g[i], MEM_HEADS)), (wmv,)],
            rows_per_batch=n_mem, tm=tm_m)
        outs["m_k"].append(mkf.reshape(bp, n_mem, MEM_HEADS, HEAD_DIM))
        outs["m_v"].append(mvf.reshape(bp, n_mem, MEM_HEADS, HEAD_DIM))
        mk_p, mv_p = mkb.reshape(bp, n_mem, mem_w), mvb.reshape(bp, n_mem, mem_w)
        mk_s = cache_mem_k[i].reshape(bs, n_mem, mem_w).astype(BF)
        mv_s = cache_mem_v[i].reshape(bs, n_mem, mem_w).astype(BF)

        if kind in (0, 2):
            if kind == 0:
                kvh, n_prev, w_in = A_KV_HEADS, A_PREV_CHUNKS, a_w_in[j]
                gq, gk, sink = a_q_norm_g[j], a_k_norm_g[j], a_sink[j]
                bias_of_rel = _t5_bias_of_rel(t5_bias)
                past_k, past_v = cache_a_k[j], cache_a_v[j]
                names = ("a_kp", "a_vp", "a_ks", "a_vs")
            else:
                kvh, n_prev, w_in = MIX_HEADS, C_PREV_CHUNKS, c_w_in[j]
                gq, gk, sink = c_q_norm_g[j], c_k_norm_g[j], None
                bias_of_rel = _clipped_bias_of_rel(c_rel_bias[j])
                past_k, past_v = cache_c_k[j], cache_c_v[j]
                names = ("c_kp", "c_vp", "c_ks", "c_vs")
            kvw = kvh * HEAD_DIM
            keep = min(n_prev * CHUNK, sp)
            nb = (n_prev * CHUNK) // LANES + 1
            wq, wk, wv, wxq, wg = _split_cols(w_in, (mix_w, kvw, kvw, mem_w, gate_w))
            weights = [(wq, _tiled_gain(gq, MIX_HEADS, attn_scale)), (wk, _tiled_gain(gk, kvh)),
                       (wv,), (wxq, g_xq), (wg,)]

            def groups(keep_rows):
                return [Group(mix_w, "headnorm", True, 0), Group(kvw, "headnorm", True, keep_rows),
                        Group(kvw, "none", True, keep_rows), Group(mem_w, "headnorm", True, 0),
                        Group(gate_w, "silu", True, 0)]

            q, k, kf, v, vf, xq, sg = _project(yp, norm_g[i], groups(keep), weights,
                                               rows_per_batch=sp, tm=tm_p)
            outs[names[0]].append(kf.reshape(bp, keep, kvh, HEAD_DIM))
            outs[names[1]].append(vf.reshape(bp, keep, kvh, HEAD_DIM))
            mix = _band_attention(q.reshape(bp, sp, mix_w), k.reshape(bp, sp, kvw),
                                  v.reshape(bp, sp, kvw), _band_bias_blocks(bias_of_rel, n_prev, nb),
                                  sink, nb=nb, kv_heads=kvh)
            yp = _merge(yp, mix.reshape(bp * sp, mix_w), xq, sg, mk_p, mv_p, w_o,
                        rows_per_batch=sp, tm=tm_p)

            q, k, kf, v, vf, xq, sg = _project(ys, norm_g[i], groups(ms), weights,
                                               rows_per_batch=ms, tm=ms)
            outs[names[2]].append(kf.reshape(bs, ss, kvh, HEAD_DIM))
            outs[names[3]].append(vf.reshape(bs, ss, kvh, HEAD_DIM))
            n_past = past_k.shape[1]
            sk_pad = _round_up(n_past + ss, LANES)
            k_all = _pad_rows(jnp.concatenate(
                [past_k.reshape(bs, n_past, kvw).astype(BF), k.reshape(bs, ss, kvw)], axis=1), sk_pad)
            v_all = _pad_rows(jnp.concatenate(
                [past_v.reshape(bs, n_past, kvw).astype(BF), v.reshape(bs, ss, kvw)], axis=1), sk_pad)
            mix = _dense_attention(q.reshape(bs, ss, mix_w), k_all, v_all,
                                   _sample_bias(bias_of_rel, n_past, ss, sk_pad), sink, kv_heads=kvh)
            ys = _merge(ys, mix.reshape(ms, mix_w), xq, sg, mk_s, mv_s, w_o,
                        rows_per_batch=ss, tm=ms)
        else:
            wcq, wckv, wkr, wxq, wg = _split_cols(
                b_w_in[j], (B_Q_RANK, B_KV_RANK, B_ROPE, mem_w, gate_w))
            wkr_rot = _rot_cols(wkr)
            wqb = b_w_q_b[j].astype(BF).reshape(B_Q_RANK, MIX_HEADS, B_NOPE + B_ROPE)
            wqb = wqb.transpose(1, 0, 2)
            wqn, wqr = wqb[..., :B_NOPE], wqb[..., B_NOPE:]
            wkvb = b_w_kv_b[j].astype(BF).reshape(B_KV_RANK, MIX_HEADS, B_NOPE + B_V)
            wkvb = wkvb.transpose(1, 0, 2)
            mla_w = (wqn, wqr, _rot_cols(wqr), wkvb[..., :B_NOPE], wkvb[..., B_NOPE:])
            gq = (b_q_norm_g[j].astype(F32) * mla_scale).reshape(1, -1)
            gk = b_k_norm_g[j].astype(F32).reshape(1, -1)
            weights = [(wcq, b_cq_norm_g[j].astype(F32).reshape(1, -1)),
                       (wckv, b_ckv_norm_g[j].astype(F32).reshape(1, -1)),
                       (wkr, wkr_rot), (wxq, g_xq), (wg,)]

            def groups(keep_rows):
                return [Group(B_Q_RANK, "fullnorm", True, 0),
                        Group(B_KV_RANK, "fullnorm", True, keep_rows),
                        Group(B_ROPE, "rope", True, keep_rows),
                        Group(mem_w, "headnorm", True, 0), Group(gate_w, "silu", True, 0)]

            cq, ckv, ckvf, kr, krf, xq, sg = _project(yp, norm_g[i], groups(sp), weights,
                                                      rows_per_batch=sp, tm=tm_p,
                                                      cos=cos_p, sin=sin_p)
            outs["b_cp"].append(ckvf.reshape(bp, sp, B_KV_RANK))
            outs["b_rp"].append(krf.reshape(bp, sp, B_ROPE))
            mix = _mla_prompt(cq.reshape(bp, sp, B_Q_RANK), ckv.reshape(bp, sp, B_KV_RANK),
                              kr.reshape(bp, sp, B_ROPE), cos_p, sin_p, mla_w, gq, gk)
            yp = _merge(yp, mix.reshape(bp * sp, mix_w), xq, sg, mk_p, mv_p, w_o,
                        rows_per_batch=sp, tm=tm_p)

            cq, _, ckvf, _, krf, xq, sg = _project(ys, norm_g[i], groups(ms), weights,
                                                   rows_per_batch=ms, tm=ms,
                                                   cos=cos_s_all, sin=sin_s_all)
            outs["b_cs"].append(ckvf.reshape(bs, ss, B_KV_RANK))
            outs["b_rs"].append(krf.reshape(bs, ss, B_ROPE))
            mix = _mla_sample(cq.reshape(bs, ss, B_Q_RANK), cache_b_ckv[j], cache_b_krope[j],
                              ckvf.reshape(bs, ss, B_KV_RANK), krf.reshape(bs, ss, B_ROPE),
                              cos_s, sin_s, mla_w, gq, gk)
            ys = _merge(ys, mix.reshape(ms, mix_w), xq, sg, mk_s, mv_s, w_o,
                        rows_per_batch=ss, tm=ms)

    st = lambda name: jnp.stack(outs[name])
    return (yp.reshape(bp, sp, d), ys.reshape(bs, ss, d),
            st("a_kp"), st("a_vp"), st("a_ks"), st("a_vs"),
            st("b_cp"), st("b_rp"), st("b_cs"), st("b_rs"),
            st("c_kp"), st("c_vp"), st("c_ks"), st("c_vs"),
            st("m_k"), st("m_v"))
```

```python
import functools
import math
from typing import NamedTuple

import numpy as np
import jax
import jax.numpy as jnp
from jax import lax
from jax.experimental import pallas as pl
from jax.experimental.pallas import tpu as pltpu

BF = jnp.bfloat16
F32 = jnp.float32

HEAD_DIM = 128
CHUNK = 64
N_MIXERS = 3
MIX_HEADS = 12
MEM_HEADS = 4
A_KV_HEADS = 4
A_PREV_CHUNKS = 2
C_PREV_CHUNKS = 8
C_REL_CLIP = 128
T5_BUCKETS = 32
T5_MAX_DIST = 128
B_Q_RANK = 512
B_KV_RANK = 256
B_NOPE = 128
B_ROPE = 64
B_V = 128
ROPE_THETA = 10000.0
EPS = 1e-6
NEG_INF = -1e30

V7X_VMEM_BYTES = 64 * 1024 * 1024
VMEM_LIMIT_BYTES = V7X_VMEM_BYTES - 8 * 1024 * 1024
LANES = 128
PROJ_TILE_ROWS = 256
MLA_Q_BLOCK = 256
MLA_KEY_CHUNK = 1024


def _params():
    return pltpu.CompilerParams(dimension_semantics=("arbitrary",),
                                vmem_limit_bytes=VMEM_LIMIT_BYTES)


def _params2():
    return pltpu.CompilerParams(dimension_semantics=("arbitrary", "arbitrary"),
                                vmem_limit_bytes=VMEM_LIMIT_BYTES)


def _resident(shape):
    nd = len(shape)
    return pl.BlockSpec(shape, lambda *_: (0,) * nd, pipeline_mode=pl.Buffered(1))


def _dot(a, b):
    return jnp.dot(a, b, preferred_element_type=F32)


def _dot_t(a, b):
    return lax.dot_general(a, b, (((1,), (1,)), ((), ())), preferred_element_type=F32)


class Group(NamedTuple):
    col: int
    width: int
    epi: str
    bf16_out: bool
    f32_keep: int

    @property
    def f32_width(self):
        return self.width // 2 if self.epi == "rope" else self.width


def _rope_fold(z, cs):
    zr = z * cs
    return zr + pltpu.roll(zr, B_ROPE, axis=1)


def _proj_body(*refs, groups, tm, tiles_per_batch, has_rope):
    it = iter(refs)
    x_ref = next(it)
    g_ref = next(it)
    w = next(it)
    cs_ref = next(it) if has_rope else None
    gains = [next(it) if gr.epi in ("headnorm", "fullnorm") else None for gr in groups]
    gout = []
    for gr in groups:
        ob = next(it) if gr.bf16_out else None
        of = next(it) if gr.f32_keep else None
        gout.append((ob, of))

    x = x_ref[...]
    ms = jnp.mean(x * x, axis=-1, keepdims=True)
    xn = (x * lax.rsqrt(ms + EPS) * g_ref[...]).astype(BF)
    t = pl.program_id(0) % tiles_per_batch

    for gr, gain, (ob, of) in zip(groups, gains, gout):
        if gr.epi in ("fullnorm", "rope"):
            chunk = gr.width
        else:
            chunk = min(gr.width, 4 * HEAD_DIM)
        for c0 in range(0, gr.width, chunk):
            z = _dot(xn, w[:, gr.col + c0:gr.col + c0 + chunk])
            zf = z
            if gr.epi == "headnorm":
                parts = []
                for h in range(chunk // HEAD_DIM):
                    zh = z[:, h * HEAD_DIM:(h + 1) * HEAD_DIM]
                    r = lax.rsqrt(jnp.mean(zh * zh, axis=-1, keepdims=True) + EPS)
                    parts.append(zh * r)
                zf = z = jnp.concatenate(parts, axis=-1) * gain[:, c0:c0 + chunk]
            elif gr.epi == "fullnorm":
                r = lax.rsqrt(jnp.mean(z * z, axis=-1, keepdims=True) + EPS)
                zf = z = z * r * gain[...]
            elif gr.epi == "rope":
                z = _rope_fold(z, cs_ref[...])
                zf = z[:, :B_ROPE]
                lane = lax.broadcasted_iota(jnp.int32, z.shape, 1)
                z = jnp.where(lane < B_ROPE, z, 0.0)
            elif gr.epi == "silu":
                zf = z = z * jax.nn.sigmoid(z)
            if ob is not None:
                ob[:, c0:c0 + chunk] = z.astype(BF)
            if of is not None:
                fw = zf.shape[1]
                f0 = c0 if fw == chunk else 0
                kb = min(gr.f32_keep, tm)
                nkb = gr.f32_keep // kb
                if kb == tm and nkb == tiles_per_batch:
                    of[:, f0:f0 + fw] = zf
                elif kb == tm:
                    @pl.when(t >= tiles_per_batch - nkb)
                    def _(of=of, zf=zf, f0=f0, fw=fw):
                        of[:, f0:f0 + fw] = zf
                else:
                    @pl.when(t == tiles_per_batch - 1)
                    def _(of=of, zf=zf, f0=f0, fw=fw, kb=kb):
                        of[:, f0:f0 + fw] = zf[tm - kb:, :]


def _project(x2, g, w, groups, gains, *, rows_per_batch, tm, cs=None):
    m, d = x2.shape
    assert m % tm == 0 and rows_per_batch % tm == 0
    tiles_per_batch = rows_per_batch // tm
    n_batches = m // rows_per_batch
    has_rope = cs is not None

    inputs = [x2, g.reshape(1, d).astype(F32), w]
    in_specs = [pl.BlockSpec((tm, d), lambda i: (i, 0)), _resident((1, d)), _resident(w.shape)]
    if has_rope:
        assert cs.shape[0] == rows_per_batch
        inputs.append(cs)
        in_specs.append(pl.BlockSpec((tm, cs.shape[1]), lambda i: (i % tiles_per_batch, 0)))
    for a in gains:
        if a is not None:
            inputs.append(a)
            in_specs.append(_resident(a.shape))

    out_shapes, out_specs = [], []
    for gr in groups:
        if gr.bf16_out:
            out_shapes.append(jax.ShapeDtypeStruct((m, gr.width), BF))
            out_specs.append(pl.BlockSpec((tm, gr.width), lambda i: (i, 0)))
        if gr.f32_keep:
            keep = gr.f32_keep
            assert keep <= rows_per_batch
            kb = min(keep, tm)
            assert keep % kb == 0 and (kb == tm or keep == kb)
            nkb = keep // kb

            def f32_map(i, nkb=nkb):
                b = i // tiles_per_batch
                t = i % tiles_per_batch
                return (b * nkb + jnp.maximum(t - (tiles_per_batch - nkb), 0), 0)

            out_shapes.append(jax.ShapeDtypeStruct((n_batches * keep, gr.f32_width), F32))
            out_specs.append(pl.BlockSpec((kb, gr.f32_width), f32_map))

    body = functools.partial(_proj_body, groups=tuple(groups), tm=tm,
                             tiles_per_batch=tiles_per_batch, has_rope=has_rope)
    return pl.pallas_call(
        body, grid=(m // tm,), in_specs=in_specs, out_specs=out_specs, out_shape=out_shapes,
        compiler_params=_params(), name="proj")(*inputs)


def _band_body(*refs, nb, heads, kv_heads, has_sink):
    if has_sink:
        q_ref, k_ref, v_ref, bias_ref, sink_ref, o_ref = refs
    else:
        q_ref, k_ref, v_ref, bias_ref, o_ref = refs
        sink_ref = None
    i = pl.program_id(1)
    start_blk = jnp.maximum(i - (nb - 1), 0)
    shift = (nb - 1) - (i - start_blk)
    row0 = pl.multiple_of(start_blk * LANES, LANES)
    group = heads // kv_heads
    bias_idx = [jnp.minimum(shift + kb, nb) for kb in range(nb)]

    scores = []
    for g in range(kv_heads):
        cols = slice(g * HEAD_DIM, (g + 1) * HEAD_DIM)
        kg = k_ref[0, pl.ds(row0, nb * LANES), cols]
        qg = [q_ref[0, :, (g * group + hh) * HEAD_DIM:(g * group + hh + 1) * HEAD_DIM]
              for hh in range(group)]
        qg = qg[0] if group == 1 else jnp.concatenate(qg, axis=0)
        bias = jnp.concatenate([bias_ref[bias_idx[kb], g] for kb in range(nb)], axis=1)
        scores.append(_dot_t(qg, kg) + bias)
    probs = []
    for g in range(kv_heads):
        s = scores[g]
        m = jnp.max(s, axis=-1, keepdims=True)
        if has_sink:
            sk = [jnp.broadcast_to(sink_ref[g * group + hh], (LANES, 1)) for hh in range(group)]
            sk = sk[0] if group == 1 else jnp.concatenate(sk, axis=0)
            m = jnp.maximum(m, sk)
        e = jnp.exp(s - m)
        den = jnp.sum(e, axis=-1, keepdims=True)
        if has_sink:
            den = den + jnp.exp(sk - m)
        probs.append((e.astype(BF), 1.0 / den))
    for g in range(kv_heads):
        e, inv = probs[g]
        vg = v_ref[0, pl.ds(row0, nb * LANES), g * HEAD_DIM:(g + 1) * HEAD_DIM]
        o = (_dot(e, vg) * inv).astype(BF)
        for hh in range(group):
            h = g * group + hh
            o_ref[0, :, h * HEAD_DIM:(h + 1) * HEAD_DIM] = o[hh * LANES:(hh + 1) * LANES, :]


def _static_take(table, idx):
    idx = np.asarray(idx)
    pieces, a = [], 0
    while a < len(idx):
        step = int(idx[a + 1] - idx[a]) if a + 1 < len(idx) else 0
        if step not in (-1, 0, 1):
            step = 0
        b = a + 1
        while b < len(idx) and idx[b] - idx[b - 1] == step:
            b += 1
        if step == 0:
            b = a + 1
            while b < len(idx) and idx[b] == idx[a]:
                b += 1
            pieces.append(jnp.broadcast_to(table[:, int(idx[a]):int(idx[a]) + 1],
                                           (table.shape[0], b - a)))
        elif step == 1:
            pieces.append(table[:, int(idx[a]):int(idx[b - 1]) + 1])
        else:
            pieces.append(jnp.flip(table[:, int(idx[b - 1]):int(idx[a]) + 1], axis=1))
        a = b
    return jnp.concatenate(pieces, axis=1)


def _toeplitz(bias_of_rel, rows, width, offset):
    n = rows + width - 1
    u = bias_of_rel(np.arange(-(width - 1), rows) + offset).astype(F32)
    v = jnp.flip(u, axis=1)
    h = v.shape[0]
    g = jnp.tile(v, (1, rows + 1))[:, :rows * (n + 1)].reshape(h, rows, n + 1)
    return jnp.flip(g[:, :, :width], axis=1)


def _band_bias_blocks(bias_of_rel, n_prev, nb, kv_heads):
    r = np.arange(LANES)[:, None]
    j = np.arange(nb * LANES)[None, :]
    k_off = j - (nb - 1) * LANES
    qc = r // CHUNK
    kc = np.floor_divide(k_off, CHUNK)
    valid = (qc - kc >= 0) & (qc - kc <= n_prev)
    b = _toeplitz(bias_of_rel, LANES, nb * LANES, (nb - 1) * LANES)
    b = jnp.where(jnp.asarray(valid)[None], b, NEG_INF)
    h = b.shape[0]
    b = b.reshape(h, LANES, nb, LANES).transpose(2, 0, 1, 3)
    b = jnp.concatenate([b, jnp.full((1, h, LANES, LANES), NEG_INF, F32)], axis=0)
    return b.reshape(nb + 1, kv_heads, (h // kv_heads) * LANES, LANES)


def _band_attention(q, k, v, bias_blocks, sink, *, nb, kv_heads):
    b, s, qw = q.shape
    kvw = k.shape[2]
    heads = qw // HEAD_DIM
    assert s % LANES == 0 and s >= nb * LANES
    inputs = [q, k, v, bias_blocks]
    in_specs = [
        pl.BlockSpec((1, LANES, qw), lambda bi, i: (bi, i, 0)),
        pl.BlockSpec((1, s, kvw), lambda bi, i: (bi, 0, 0)),
        pl.BlockSpec((1, s, kvw), lambda bi, i: (bi, 0, 0)),
        _resident(bias_blocks.shape),
    ]
    if sink is not None:
        inputs.append(sink.reshape(heads, 1, 1).astype(F32))
        in_specs.append(_resident((heads, 1, 1)))
    body = functools.partial(_band_body, nb=nb, heads=heads, kv_heads=kv_heads,
                             has_sink=sink is not None)
    return pl.pallas_call(
        body, grid=(b, s // LANES), in_specs=in_specs,
        out_specs=pl.BlockSpec((1, LANES, qw), lambda bi, i: (bi, i, 0)),
        out_shape=jax.ShapeDtypeStruct((b, s, qw), BF),
        compiler_params=_params2(), name="band_attn")(*inputs)


def _dense_body(*refs, heads, kv_heads, has_sink):
    if has_sink:
        q_ref, k_ref, v_ref, bias_ref, sink_ref, o_ref = refs
    else:
        q_ref, k_ref, v_ref, bias_ref, o_ref = refs
        sink_ref = None
    group = heads // kv_heads
    for g in range(kv_heads):
        kg = k_ref[0, :, g * HEAD_DIM:(g + 1) * HEAD_DIM]
        vg = v_ref[0, :, g * HEAD_DIM:(g + 1) * HEAD_DIM]
        for hh in range(group):
            h = g * group + hh
            qh = q_ref[0, :, h * HEAD_DIM:(h + 1) * HEAD_DIM]
            s = _dot_t(qh, kg) + bias_ref[h]
            m = jnp.max(s, axis=-1, keepdims=True)
            if has_sink:
                sk = sink_ref[h]
                m = jnp.maximum(m, sk)
            e = jnp.exp(s - m)
            den = jnp.sum(e, axis=-1, keepdims=True)
            if has_sink:
                den = den + jnp.exp(sk - m)
            o = _dot(e.astype(BF), vg) * (1.0 / den)
            o_ref[0, :, h * HEAD_DIM:(h + 1) * HEAD_DIM] = o.astype(BF)


def _dense_attention(q, k, v, bias, sink, *, kv_heads):
    b, sq, qw = q.shape
    sk, kvw = k.shape[1], k.shape[2]
    heads = qw // HEAD_DIM
    inputs = [q, k, v, bias]
    in_specs = [
        pl.BlockSpec((1, sq, qw), lambda bi: (bi, 0, 0)),
        pl.BlockSpec((1, sk, kvw), lambda bi: (bi, 0, 0)),
        pl.BlockSpec((1, sk, kvw), lambda bi: (bi, 0, 0)),
        _resident(bias.shape),
    ]
    if sink is not None:
        inputs.append(sink.reshape(heads, 1, 1).astype(F32))
        in_specs.append(_resident((heads, 1, 1)))
    body = functools.partial(_dense_body, heads=heads, kv_heads=kv_heads,
                             has_sink=sink is not None)
    return pl.pallas_call(
        body, grid=(b,), in_specs=in_specs,
        out_specs=pl.BlockSpec((1, sq, qw), lambda bi: (bi, 0, 0)),
        out_shape=jax.ShapeDtypeStruct((b, sq, qw), BF),
        compiler_params=_params(), name="dense_attn")(*inputs)


def _mla_queries(cq, wq, cs, gq):
    z = _dot(cq, wq)
    qn = z[:, :B_NOPE]
    qr = _rope_fold(z[:, B_NOPE:], cs)
    ms = (jnp.sum(qn * qn, axis=-1, keepdims=True)
          + 0.5 * jnp.sum(qr * qr, axis=-1, keepdims=True)) * (1.0 / (B_NOPE + B_ROPE))
    r = lax.rsqrt(ms + EPS)
    return jnp.concatenate([qn * r * gq[:, :B_NOPE], qr * r * gq[:, B_NOPE:]], axis=-1).astype(BF)


def _mla_keys_values(ckv, kr, wkv, gk):
    z = _dot(ckv, wkv)
    kn = z[:, :B_NOPE]
    ms = (jnp.sum(kn * kn, axis=-1, keepdims=True)
          + jnp.sum(kr * kr, axis=-1, keepdims=True)) * (1.0 / (B_NOPE + B_ROPE))
    r = lax.rsqrt(ms + EPS)
    k = jnp.concatenate([kn * r * gk[:, :B_NOPE], kr * r * gk[:, B_NOPE:]], axis=-1).astype(BF)
    return k, z[:, B_NOPE:].astype(BF)


def _mla_prompt_body(cq_ref, ckv_ref, kr_ref, cs_ref, wq_ref, wkv_ref, gq_ref, gk_ref, o_ref,
                     q_s, k_s, v_s, *, seq):
    qb = MLA_Q_BLOCK
    for r0 in range(0, seq, qb):
        rows = slice(r0, r0 + qb)
        q_s[rows, :] = _mla_queries(cq_ref[0, rows, :], wq_ref[...], cs_ref[rows, :], gq_ref[...])
        k, v = _mla_keys_values(ckv_ref[0, rows, :], kr_ref[0, rows, :].astype(F32),
                                wkv_ref[...], gk_ref[...])
        k_s[rows, :] = k
        v_s[rows, :] = v
    chunk_shift = CHUNK.bit_length() - 1
    ri = lax.shift_right_logical(lax.broadcasted_iota(jnp.int32, (qb, qb), 0), chunk_shift)
    ci = lax.shift_right_logical(lax.broadcasted_iota(jnp.int32, (qb, qb), 1), chunk_shift)
    diag_bias = jnp.where(ci <= ri, 0.0, NEG_INF).astype(F32)
    for r0 in range(0, seq, qb):
        rows = slice(r0, r0 + qb)
        q = q_s[rows, :]
        s_d = _dot_t(q, k_s[rows, :]) + diag_bias
        m = jnp.max(s_d, axis=-1, keepdims=True)
        if r0 > 0:
            s_o = _dot_t(q, k_s[0:r0, :])
            m = jnp.maximum(m, jnp.max(s_o, axis=-1, keepdims=True))
        e_d = jnp.exp(s_d - m)
        den = jnp.sum(e_d, axis=-1, keepdims=True)
        o = _dot(e_d.astype(BF), v_s[rows, :])
        if r0 > 0:
            e_o = jnp.exp(s_o - m)
            den = den + jnp.sum(e_o, axis=-1, keepdims=True)
            o = o + _dot(e_o.astype(BF), v_s[0:r0, :])
        o_ref[0, rows, :] = (o * (1.0 / den)).astype(BF)


MLA_QK_DIM = 2 * HEAD_DIM


def _mla_weight_specs(wq, wkv):
    return [pl.BlockSpec((wq.shape[0], MLA_QK_DIM), lambda bi, h: (0, h)),
            pl.BlockSpec((wkv.shape[0], B_NOPE + B_V), lambda bi, h: (0, h))]


def _mla_prompt(cq, ckv, kr, cs, wq, wkv, gq, gk):
    b, s, _ = cq.shape
    assert s % MLA_Q_BLOCK == 0
    per_batch = lambda a: pl.BlockSpec((1,) + a.shape[1:], lambda bi, h: (bi, 0, 0))
    in_specs = [per_batch(cq), per_batch(ckv), per_batch(kr), _resident(cs.shape),
                *_mla_weight_specs(wq, wkv), _resident(gq.shape), _resident(gk.shape)]
    return pl.pallas_call(
        functools.partial(_mla_prompt_body, seq=s),
        grid=(b, MIX_HEADS), in_specs=in_specs,
        out_specs=pl.BlockSpec((1, s, B_V), lambda bi, h: (bi, 0, h)),
        out_shape=jax.ShapeDtypeStruct((b, s, MIX_HEADS * B_V), BF),
        scratch_shapes=[pltpu.VMEM((s, MLA_QK_DIM), BF), pltpu.VMEM((s, MLA_QK_DIM), BF),
                        pltpu.VMEM((s, B_V), BF)],
        compiler_params=_params2(), name="mla_prompt")(cq, ckv, kr, cs, wq, wkv, gq, gk)


def _mla_sample_body(cq_ref, pckv_ref, pkr_ref, nckv_ref, nkr_ref, cs_ref, wq_ref, wkv_ref,
                     gq_ref, gk_ref, o_ref, *, past):
    q = _mla_queries(cq_ref[0], wq_ref[...], cs_ref[...], gq_ref[...])
    pieces = []
    for r0 in range(0, past, MLA_KEY_CHUNK):
        rows = slice(r0, min(r0 + MLA_KEY_CHUNK, past))
        k, v = _mla_keys_values(pckv_ref[0, rows, :].astype(BF), pkr_ref[0, rows, :],
                                wkv_ref[...], gk_ref[...])
        pieces.append((_dot_t(q, k), v))
    k, v = _mla_keys_values(nckv_ref[0].astype(BF), nkr_ref[0], wkv_ref[...], gk_ref[...])
    pieces.append((_dot_t(q, k), v))
    m = functools.reduce(jnp.maximum, [jnp.max(s, axis=-1, keepdims=True) for s, _ in pieces])
    den = 0.0
    o = 0.0
    for s, v in pieces:
        e = jnp.exp(s - m)
        den = den + jnp.sum(e, axis=-1, keepdims=True)
        o = o + _dot(e.astype(BF), v)
    o_ref[0] = (o * (1.0 / den)).astype(BF)


def _mla_sample(cq, past_ckv, past_kr, new_ckv, new_kr, cs, wq, wkv, gq, gk):
    b, n, _ = cq.shape
    past = past_ckv.shape[1]
    per_batch = lambda a: pl.BlockSpec((1,) + a.shape[1:], lambda bi, h: (bi, 0, 0))
    in_specs = [per_batch(cq), per_batch(past_ckv), per_batch(past_kr), per_batch(new_ckv),
                per_batch(new_kr), _resident(cs.shape), *_mla_weight_specs(wq, wkv),
                _resident(gq.shape), _resident(gk.shape)]
    return pl.pallas_call(
        functools.partial(_mla_sample_body, past=past),
        grid=(b, MIX_HEADS), in_specs=in_specs,
        out_specs=pl.BlockSpec((1, n, B_V), lambda bi, h: (bi, 0, h)),
        out_shape=jax.ShapeDtypeStruct((b, n, MIX_HEADS * B_V), BF),
        compiler_params=_params2(), name="mla_sample")(cq, past_ckv, past_kr, new_ckv, new_kr, cs,
                                                        wq, wkv, gq, gk)


def _merge_body(x_ref, mix_ref, xq_ref, sg_ref, mk_ref, mv_ref, w_ref, y_ref, *, nbat, rows):
    mix_w = mix_ref.shape[1]
    outs = []
    for bi in range(nbat):
        rs = slice(bi * rows, (bi + 1) * rows)
        heads = []
        for h in range(MEM_HEADS):
            cs = slice(h * HEAD_DIM, (h + 1) * HEAD_DIM)
            s = _dot_t(xq_ref[rs, cs], mk_ref[bi, :, cs])
            m = jnp.max(s, axis=-1, keepdims=True)
            e = jnp.exp(s - m)
            den = jnp.sum(e, axis=-1, keepdims=True)
            heads.append(_dot(e.astype(BF), mv_ref[bi, :, cs]) * (1.0 / den))
        outs.append(jnp.concatenate(heads, axis=-1))
    xo = outs[0] if nbat == 1 else jnp.concatenate(outs, axis=0)
    o1 = mix_ref[...] * sg_ref[:, :mix_w]
    o2 = xo.astype(BF) * sg_ref[:, mix_w:]
    y_ref[...] = x_ref[...] + _dot(o1, w_ref[:mix_w, :]) + _dot(o2, w_ref[mix_w:, :])


def _merge(x2, mix, xq, sg, mk, mv, w_out, *, rows_per_batch, tm):
    m, d = x2.shape
    if tm <= rows_per_batch:
        assert rows_per_batch % tm == 0
        tiles_per_batch = rows_per_batch // tm
        nbat, rows = 1, tm
        mem_map = lambda i: (i // tiles_per_batch, 0, 0)
    else:
        assert tm % rows_per_batch == 0
        nbat, rows = tm // rows_per_batch, rows_per_batch
        mem_map = lambda i: (i, 0, 0)
    row_spec = lambda a: pl.BlockSpec((tm, a.shape[1]), lambda i: (i, 0))
    mem_spec = pl.BlockSpec((nbat,) + mk.shape[1:], mem_map)
    return pl.pallas_call(
        functools.partial(_merge_body, nbat=nbat, rows=rows),
        grid=(m // tm,),
        in_specs=[row_spec(x2), row_spec(mix), row_spec(xq), row_spec(sg), mem_spec, mem_spec,
                  _resident(w_out.shape)],
        out_specs=pl.BlockSpec((tm, d), lambda i: (i, 0)),
        out_shape=jax.ShapeDtypeStruct((m, d), F32),
        compiler_params=_params(), name="merge")(x2, mix, xq, sg, mk, mv, w_out)


def _col_groups(specs):
    groups, col = [], 0
    for width, epi, bf16_out, f32_keep in specs:
        groups.append(Group(col, width, epi, bf16_out, f32_keep))
        col += width
    return groups


def _tiled_gain(g, heads, scale=1.0):
    return (jnp.tile(g.astype(F32), heads) * scale).reshape(1, -1)


def _t5_bucket_static(rel):
    half = T5_BUCKETS // 2
    max_exact = half // 2
    ret = np.where(rel < 0, half, 0)
    n = np.abs(rel)
    nf = np.maximum(n, 1).astype(np.float64)
    large = max_exact + (np.log(nf / max_exact) / math.log(T5_MAX_DIST / max_exact)
                         * (half - max_exact)).astype(np.int32)
    large = np.minimum(large, half - 1)
    return ret + np.where(n < max_exact, n, large)


def _t5_bias_of_rel(table):
    return lambda rel: _static_take(table.T, _t5_bucket_static(rel))


def _clipped_bias_of_rel(table):
    return lambda rel: _static_take(table, np.clip(rel, -C_REL_CLIP, C_REL_CLIP) + C_REL_CLIP)


def _rope_table(pos):
    half = B_ROPE // 2
    inv = ROPE_THETA ** (-jnp.arange(half, dtype=F32) / half)
    ang = pos.astype(F32)[:, None] * inv[None, :]
    cos, sin = jnp.cos(ang), jnp.sin(ang)
    return jnp.concatenate([cos, cos, sin, sin], axis=-1)


def _rot_cols(w):
    shape = w.shape
    w = w.reshape(shape[:-1] + (-1, 2, B_ROPE // 2))
    w = jnp.stack([-w[..., 1, :], w[..., 0, :]], axis=-2)
    return w.reshape(shape)


def _sample_bias(bias_of_rel, past, n, pad_to):
    b = _toeplitz(bias_of_rel, n, past + n, past)
    return jnp.pad(b, ((0, 0), (0, 0), (0, pad_to - (past + n))), constant_values=NEG_INF)


def _pad_rows(a, rows):
    return jnp.pad(a, ((0, 0), (0, rows - a.shape[1]), (0, 0)))


def _round_up(n, m):
    return (n + m - 1) // m * m


def kernel(x_prompt, x_sample, mem_prompt, cache_a_k, cache_a_v, cache_b_ckv, cache_b_krope, cache_c_k, cache_c_v, cache_mem_k, cache_mem_v, t5_bias, norm_g, w_out, mem_norm_g, w_mem_kv, xq_norm_g, xk_norm_g, a_w_in, a_q_norm_g, a_k_norm_g, a_sink, b_w_in, b_cq_norm_g, b_w_q_b, b_ckv_norm_g, b_w_kv_b, b_q_norm_g, b_k_norm_g, c_w_in, c_q_norm_g, c_k_norm_g, c_rel_bias):
    bp, sp, d = x_prompt.shape
    bs, ss, _ = x_sample.shape
    depth = norm_g.shape[0]
    n_mem = mem_prompt.shape[1]
    past = cache_b_ckv.shape[2]
    mix_w = MIX_HEADS * HEAD_DIM
    mem_w = MEM_HEADS * HEAD_DIM
    gate_w = mix_w + mem_w
    attn_scale = HEAD_DIM ** -0.5
    mla_scale = (B_NOPE + B_ROPE) ** -0.5
    tm_p = min(PROJ_TILE_ROWS, sp)
    tm_m = min(PROJ_TILE_ROWS, n_mem)
    ms = bs * ss

    yp = x_prompt.reshape(bp * sp, d)
    ys = x_sample.reshape(ms, d)
    mem2 = mem_prompt.reshape(bp * n_mem, d)

    cs_p = _rope_table(jnp.arange(sp))
    cs_s = _rope_table(past + jnp.arange(ss))
    cs_s_all = jnp.tile(cs_s, (bs, 1))

    outs = {k: [] for k in ("a_kp", "a_vp", "a_ks", "a_vs", "b_cp", "b_rp", "b_cs", "b_rs",
                            "c_kp", "c_vp", "c_ks", "c_vs", "m_k", "m_v")}

    for i in range(depth):
        kind, j = i % N_MIXERS, i // N_MIXERS
        w_o = w_out[i].astype(BF)
        g_xq = _tiled_gain(xq_norm_g[i], MEM_HEADS, attn_scale)

        mem_groups = _col_groups([(mem_w, "headnorm", True, n_mem), (mem_w, "none", True, n_mem)])
        mkb, mkf, mvb, mvf = _project(
            mem2, mem_norm_g[i], w_mem_kv[i].astype(BF), mem_groups,
            [_tiled_gain(xk_norm_g[i], MEM_HEADS), None], rows_per_batch=n_mem, tm=tm_m)
        outs["m_k"].append(mkf.reshape(bp, n_mem, MEM_HEADS, HEAD_DIM))
        outs["m_v"].append(mvf.reshape(bp, n_mem, MEM_HEADS, HEAD_DIM))
        mk_p, mv_p = mkb.reshape(bp, n_mem, mem_w), mvb.reshape(bp, n_mem, mem_w)
        mk_s = cache_mem_k[i].reshape(bs, n_mem, mem_w).astype(BF)
        mv_s = cache_mem_v[i].reshape(bs, n_mem, mem_w).astype(BF)

        if kind in (0, 2):
            if kind == 0:
                kvh, n_prev, w_in = A_KV_HEADS, A_PREV_CHUNKS, a_w_in[j]
                gq, gk, sink = a_q_norm_g[j], a_k_norm_g[j], a_sink[j]
                bias_of_rel = _t5_bias_of_rel(t5_bias)
                past_k, past_v = cache_a_k[j], cache_a_v[j]
                names = ("a_kp", "a_vp", "a_ks", "a_vs")
            else:
                kvh, n_prev, w_in = MIX_HEADS, C_PREV_CHUNKS, c_w_in[j]
                gq, gk, sink = c_q_norm_g[j], c_k_norm_g[j], None
                bias_of_rel = _clipped_bias_of_rel(c_rel_bias[j])
                past_k, past_v = cache_c_k[j], cache_c_v[j]
                names = ("c_kp", "c_vp", "c_ks", "c_vs")
            kvw = kvh * HEAD_DIM
            keep = min(n_prev * CHUNK, sp)
            nb = (n_prev * CHUNK) // LANES + 1
            w_bf = w_in.astype(BF)
            gains = [_tiled_gain(gq, MIX_HEADS, attn_scale), _tiled_gain(gk, kvh), None, g_xq, None]

            def groups(keep_rows):
                return _col_groups([(mix_w, "headnorm", True, 0), (kvw, "headnorm", True, keep_rows),
                                    (kvw, "none", True, keep_rows), (mem_w, "headnorm", True, 0),
                                    (gate_w, "silu", True, 0)])

            q, k, kf, v, vf, xq, sg = _project(yp, norm_g[i], w_bf, groups(keep), gains,
                                               rows_per_batch=sp, tm=tm_p)
            outs[names[0]].append(kf.reshape(bp, keep, kvh, HEAD_DIM))
            outs[names[1]].append(vf.reshape(bp, keep, kvh, HEAD_DIM))
            mix = _band_attention(q.reshape(bp, sp, mix_w), k.reshape(bp, sp, kvw),
                                  v.reshape(bp, sp, kvw), _band_bias_blocks(bias_of_rel, n_prev, nb, kvh),
                                  sink, nb=nb, kv_heads=kvh)
            yp = _merge(yp, mix.reshape(bp * sp, mix_w), xq, sg, mk_p, mv_p, w_o,
                        rows_per_batch=sp, tm=tm_p)

            q, k, kf, v, vf, xq, sg = _project(ys, norm_g[i], w_bf, groups(ms), gains,
                                               rows_per_batch=ms, tm=ms)
            outs[names[2]].append(kf.reshape(bs, ss, kvh, HEAD_DIM))
            outs[names[3]].append(vf.reshape(bs, ss, kvh, HEAD_DIM))
            n_past = past_k.shape[1]
            sk_pad = _round_up(n_past + ss, LANES)
            k_all = _pad_rows(jnp.concatenate(
                [past_k.reshape(bs, n_past, kvw).astype(BF), k.reshape(bs, ss, kvw)], axis=1), sk_pad)
            v_all = _pad_rows(jnp.concatenate(
                [past_v.reshape(bs, n_past, kvw).astype(BF), v.reshape(bs, ss, kvw)], axis=1), sk_pad)
            mix = _dense_attention(q.reshape(bs, ss, mix_w), k_all, v_all,
                                   _sample_bias(bias_of_rel, n_past, ss, sk_pad), sink, kv_heads=kvh)
            ys = _merge(ys, mix.reshape(ms, mix_w), xq, sg, mk_s, mv_s, w_o,
                        rows_per_batch=ss, tm=ms)
        else:
            w_in = b_w_in[j]
            r0, r1 = B_Q_RANK + B_KV_RANK, B_Q_RANK + B_KV_RANK + B_ROPE
            w_bf = jnp.concatenate([w_in[:, :r0], w_in[:, r1:], w_in[:, r0:r1],
                                    _rot_cols(w_in[:, r0:r1])], axis=1).astype(BF)
            wq = b_w_q_b[j].reshape(B_Q_RANK, MIX_HEADS, B_NOPE + B_ROPE)
            wq = jnp.concatenate([wq, _rot_cols(wq[..., B_NOPE:])], axis=-1)
            wq = wq.reshape(B_Q_RANK, MIX_HEADS * MLA_QK_DIM).astype(BF)
            wkv = b_w_kv_b[j].astype(BF)
            gq, gk = b_q_norm_g[j].astype(F32), b_k_norm_g[j].astype(F32)
            gq = (jnp.concatenate([gq, gq[B_NOPE:]]) * mla_scale).reshape(1, -1)
            gk = jnp.concatenate([gk, jnp.zeros((B_ROPE,), F32)]).reshape(1, -1)
            gains = [b_cq_norm_g[j].astype(F32).reshape(1, -1),
                     b_ckv_norm_g[j].astype(F32).reshape(1, -1), g_xq, None, None]

            def groups(keep_rows):
                return _col_groups([(B_Q_RANK, "fullnorm", True, 0),
                                    (B_KV_RANK, "fullnorm", True, keep_rows),
                                    (mem_w, "headnorm", True, 0), (gate_w, "silu", True, 0),
                                    (2 * B_ROPE, "rope", True, keep_rows)])

            cq, ckv, ckvf, xq, sg, kr, krf = _project(yp, norm_g[i], w_bf, groups(sp), gains,
                                                      rows_per_batch=sp, tm=tm_p, cs=cs_p)
            outs["b_cp"].append(ckvf.reshape(bp, sp, B_KV_RANK))
            outs["b_rp"].append(krf.reshape(bp, sp, B_ROPE))
            mix = _mla_prompt(cq.reshape(bp, sp, B_Q_RANK), ckv.reshape(bp, sp, B_KV_RANK),
                              kr.reshape(bp, sp, 2 * B_ROPE), cs_p, wq, wkv, gq, gk)
            yp = _merge(yp, mix.reshape(bp * sp, mix_w), xq, sg, mk_p, mv_p, w_o,
                        rows_per_batch=sp, tm=tm_p)

            cq, _, ckvf, xq, sg, _, krf = _project(ys, norm_g[i], w_bf, groups(ms), gains,
                                                   rows_per_batch=ms, tm=ms, cs=cs_s_all)
            outs["b_cs"].append(ckvf.reshape(bs, ss, B_KV_RANK))
            outs["b_rs"].append(krf.reshape(bs, ss, B_ROPE))
            pad_lanes = lambda a: jnp.pad(a, ((0, 0), (0, 0), (0, B_ROPE)))
            mix = _mla_sample(cq.reshape(bs, ss, B_Q_RANK), cache_b_ckv[j],
                              pad_lanes(cache_b_krope[j]), ckvf.reshape(bs, ss, B_KV_RANK),
                              pad_lanes(krf.reshape(bs, ss, B_ROPE)), cs_s, wq, wkv, gq, gk)
            ys = _merge(ys, mix.reshape(ms, mix_w), xq, sg, mk_s, mv_s, w_o,
                        rows_per_batch=ss, tm=ms)

    st = lambda name: jnp.stack(outs[name])
    return (yp.reshape(bp, sp, d), ys.reshape(bs, ss, d),
            st("a_kp"), st("a_vp"), st("a_ks"), st("a_vs"),
            st("b_cp"), st("b_rp"), st("b_cs"), st("b_rs"),
            st("c_kp"), st("c_vp"), st("c_ks"), st("c_vs"),
            st("m_k"), st("m_v"))
```
